```python
import math
import jax, jax.numpy as jnp
from jax import lax
import numpy as np

D_MODEL = 2048
BATCH = 4
SEQ = 4096
DEPTH = 1

CHUNK = 64
D_MIX = 2 * D_MODEL
D_SSD = D_MIX // 2
D_LRU = D_MIX - D_SSD
SSD_HEAD_DIM = 64
SSD_HEADS = D_SSD // SSD_HEAD_DIM
SSD_GROUPS = 8
SSD_STATE = 128
D_XBC = D_SSD + 2 * SSD_GROUPS * SSD_STATE
LRU_HEADS = 16
LRU_BLOCK = D_LRU // LRU_HEADS
LRU_C = 8.0
CONV_WIDTH = 4
D_FF = 4 * D_MODEL
D_IN = D_SSD + D_XBC + SSD_HEADS + 2 * D_LRU
EPS = 1e-6

kernel_name = "hybrid_ssd_rglru_parallel_block"


def rms_norm(x, g):
    xf = x.astype(jnp.float32)
    y = xf * lax.rsqrt(jnp.mean(xf * xf, axis=-1, keepdims=True) + EPS)
    return (y * g.astype(jnp.float32)).astype(x.dtype)


def causal_dwconv(x, w, b):
    c = x.shape[-1]
    y = lax.conv_general_dilated(
        x, w[:, None, :].astype(x.dtype), window_strides=(1,),
        padding=[(w.shape[0] - 1, 0)],
        dimension_numbers=("NWC", "WIO", "NWC"), feature_group_count=c)
    return y + b.astype(x.dtype)


def ssd_scan(xh, dt, a, bm, cm, d_skip):
    b_, l_, h_, p_ = xh.shape
    g_, n_ = bm.shape[2], bm.shape[3]
    k_ = h_ // g_
    c_ = l_ // CHUNK
    x6 = (xh * dt[..., None]).reshape(b_, c_, CHUNK, g_, k_, p_)
    adt = (dt * a).reshape(b_, c_, CHUNK, g_, k_)
    acs = jnp.cumsum(adt, axis=2)
    bc = bm.reshape(b_, c_, CHUNK, g_, n_)
    cc = cm.reshape(b_, c_, CHUNK, g_, n_)
    causal = jnp.tril(jnp.ones((CHUNK, CHUNK), dtype=bool))[None, None, :, :, None, None]
    seg = acs[:, :, :, None] - acs[:, :, None, :]
    decay = jnp.exp(jnp.where(causal, seg, -jnp.inf))
    scores = jnp.einsum("bclgn,bcsgn->bclsg", cc, bc)
    y_diag = jnp.einsum("bclsg,bclsgk,bcsgkp->bclgkp", scores, decay, x6)
    decay_to_end = jnp.exp(acs[:, :, -1:] - acs)
    states = jnp.einsum("bcsgn,bcsgk,bcsgkp->bcgkpn", bc, decay_to_end, x6)
    chunk_decay = jnp.exp(acs[:, :, -1])

    def step(h, inp):
        s, dcy = inp
        return h * dcy[..., None, None] + s, h

    h0 = jnp.zeros((b_, g_, k_, p_, n_), dtype=xh.dtype)
    _, prev = lax.scan(step, h0, (jnp.moveaxis(states, 1, 0), jnp.moveaxis(chunk_decay, 1, 0)))
    prev = jnp.moveaxis(prev, 0, 1)
    y_off = jnp.einsum("bclgn,bcgkpn,bclgk->bclgkp", cc, prev, jnp.exp(acs))
    y = (y_diag + y_off).reshape(b_, l_, h_, p_)
    return y + xh * d_skip[:, None]


def rg_lru(xl, w_a, b_a, w_x, b_x, lam):
    b_, l_, _ = xl.shape
    xb = xl.reshape(b_, l_, LRU_HEADS, LRU_BLOCK)
    r = jax.nn.sigmoid(jnp.einsum("blhi,hij->blhj", xb, w_a) + b_a).reshape(b_, l_, D_LRU)
    i = jax.nn.sigmoid(jnp.einsum("blhi,hij->blhj", xb, w_x) + b_x).reshape(b_, l_, D_LRU)
    log_a = -LRU_C * r * jax.nn.softplus(-lam)
    a = jnp.exp(log_a)
    u = jnp.sqrt(-jnp.expm1(2.0 * log_a)) * (i * xl)

    def combine(e1, e2):
        a1, b1 = e1
        a2, b2 = e2
        return a1 * a2, a2 * b1 + b2

    _, h = lax.associative_scan(combine, (a, u), axis=1)
    return h


def setup_inputs(seed: int = 0) -> dict:
    key = jax.random.key(seed)
    ks = jax.random.split(key, 32)
    f32 = jnp.float32
    nrm = lambda k, s, sc: jax.random.normal(k, s, f32) * sc
    gain = lambda k, s: 1.0 + 0.02 * jax.random.normal(k, s, f32)
    L = DEPTH
    dt0 = jnp.exp(jax.random.uniform(ks[10], (L, SSD_HEADS), f32, math.log(1e-3), math.log(1e-1)))
    a0 = jax.random.uniform(ks[14], (L, D_LRU), f32, 0.9, 0.999)
    a_base = jnp.exp(jnp.log(a0) / LRU_C)
    return {
        "x": jax.random.normal(ks[0], (BATCH, SEQ, D_MODEL), f32),
        "pre_mix_norm": gain(ks[1], (L, D_MODEL)),
        "w_in": nrm(ks[2], (L, D_MODEL, D_IN), D_MODEL ** -0.5),
        "ssd_conv_w": nrm(ks[3], (L, CONV_WIDTH, D_XBC), CONV_WIDTH ** -0.5),
        "ssd_conv_b": nrm(ks[4], (L, D_XBC), 0.01),
        "ssd_dt_bias": dt0 + jnp.log(-jnp.expm1(-dt0)),
        "ssd_a_log": jnp.log(jax.random.uniform(ks[11], (L, SSD_HEADS), f32, 1.0, 16.0)),
        "ssd_d": gain(ks[12], (L, SSD_HEADS)),
        "ssd_norm": gain(ks[13], (L, D_SSD)),
        "lru_conv_w": nrm(ks[5], (L, CONV_WIDTH, D_LRU), CONV_WIDTH ** -0.5),
        "lru_conv_b": nrm(ks[6], (L, D_LRU), 0.01),
        "lru_w_a": nrm(ks[7], (L, LRU_HEADS, LRU_BLOCK, LRU_BLOCK), LRU_BLOCK ** -0.5),
        "lru_b_a": nrm(ks[8], (L, LRU_HEADS, LRU_BLOCK), 0.01),
        "lru_w_x": nrm(ks[9], (L, LRU_HEADS, LRU_BLOCK, LRU_BLOCK), LRU_BLOCK ** -0.5),
        "lru_b_x": nrm(ks[15], (L, LRU_HEADS, LRU_BLOCK), 0.01),
        "lru_lambda": jnp.log(a_base) - jnp.log1p(-a_base),
        "lru_norm": gain(ks[16], (L, D_LRU)),
        "w_out": nrm(ks[17], (L, D_MIX, D_MODEL), D_MIX ** -0.5),
        "post_mix_norm": gain(ks[18], (L, D_MODEL)),
        "pre_mlp_norm": gain(ks[19], (L, D_MODEL)),
        "w_mlp_in": nrm(ks[20], (L, D_MODEL, D_FF), D_MODEL ** -0.5),
        "w_mlp_out": nrm(ks[21], (L, D_FF, D_MODEL), D_FF ** -0.5),
        "post_mlp_norm": gain(ks[22], (L, D_MODEL)),
    }


def reference(x, pre_mix_norm, w_in, ssd_conv_w, ssd_conv_b, ssd_dt_bias, ssd_a_log, ssd_d,
              ssd_norm, lru_conv_w, lru_conv_b, lru_w_a, lru_b_a, lru_w_x, lru_b_x, lru_lambda,
              lru_norm, w_out, post_mix_norm, pre_mlp_norm, w_mlp_in, w_mlp_out, post_mlp_norm):
    f32 = jnp.float32
    b_, l_, _ = x.shape
    split_at = np.cumsum([D_SSD, D_XBC, SSD_HEADS, D_LRU]).tolist()
    for li in range(DEPTH):
        h = rms_norm(x, pre_mix_norm[li])
        proj = h @ w_in[li].astype(h.dtype)
        z, xbc, dt_raw, gate_lru, x_lru = jnp.split(proj, split_at, axis=-1)

        xbc = jax.nn.silu(causal_dwconv(xbc, ssd_conv_w[li], ssd_conv_b[li])).astype(f32)
        xs, bm, cm = jnp.split(xbc, [D_SSD, D_SSD + SSD_GROUPS * SSD_STATE], axis=-1)
        dt = jax.nn.softplus(dt_raw.astype(f32) + ssd_dt_bias[li].astype(f32))
        a = -jnp.exp(ssd_a_log[li].astype(f32))
        y_ssd = ssd_scan(xs.reshape(b_, l_, SSD_HEADS, SSD_HEAD_DIM), dt, a,
                         bm.reshape(b_, l_, SSD_GROUPS, SSD_STATE),
                         cm.reshape(b_, l_, SSD_GROUPS, SSD_STATE), ssd_d[li].astype(f32))
        y_ssd = y_ssd.reshape(b_, l_, D_SSD) * jax.nn.silu(z.astype(f32))
        yg = y_ssd.reshape(b_, l_, SSD_GROUPS, D_SSD // SSD_GROUPS)
        yg = yg * lax.rsqrt(jnp.mean(yg * yg, axis=-1, keepdims=True) + EPS)
        y_ssd = (yg.reshape(b_, l_, D_SSD) * ssd_norm[li].astype(f32)).astype(x.dtype)

        xl = causal_dwconv(x_lru, lru_conv_w[li], lru_conv_b[li]).astype(f32)
        hl = rg_lru(xl, lru_w_a[li].astype(f32), lru_b_a[li].astype(f32),
                    lru_w_x[li].astype(f32), lru_b_x[li].astype(f32), lru_lambda[li].astype(f32))
        y_lru = rms_norm(hl * jax.nn.gelu(gate_lru.astype(f32)), lru_norm[li]).astype(x.dtype)

        mix = jnp.concatenate([y_ssd, y_lru], axis=-1) @ w_out[li].astype(x.dtype)
        x = x + rms_norm(mix, post_mix_norm[li])

        hm = rms_norm(x, pre_mlp_norm[li]) @ w_mlp_in[li].astype(x.dtype)
        hm = jnp.square(jax.nn.relu(hm)) @ w_mlp_out[li].astype(x.dtype)
        x = x + rms_norm(hm, post_mlp_norm[li])
    return x
```

```python
import functools

import jax
import jax.numpy as jnp
from jax import lax
from jax.experimental import pallas as pl
from jax.experimental.pallas import tpu as pltpu

F32 = jnp.float32
BF16 = jnp.bfloat16

D_MODEL = 2048
D_SSD = 2048
D_LRU = 2048
SSD_HEAD_DIM = 64
SSD_HEADS = 32
SSD_GROUPS = 8
SSD_STATE = 128
HEADS_PER_GROUP = SSD_HEADS // SSD_GROUPS
GROUP_WIDTH = D_SSD // SSD_GROUPS
LRU_HEADS = 16
LRU_BLOCK = 128
LRU_C = 8.0
CONV_WIDTH = 4
D_FF = 8192
EPS = 1e-6

LANES = 128
SUBLANES = 8
VMEM_LIMIT = 56 * 1024 * 1024

COL_Z, COL_XS, COL_BC, COL_GATE, COL_XL = range(5)
N_MAIN = 5 * D_MODEL

SSD_CHUNK = 128
LRU_TIME = 256

IN_TM, IN_TN = 1024, 1024
OUT_TM = 512
MLP_TM, MLP_TF = 512, 1024


def _sigmoid(v):
    return 1.0 / (1.0 + jnp.exp(-v))


def _softplus(v):
    return jnp.maximum(v, 0.0) + jnp.log1p(jnp.exp(-jnp.abs(v)))


def _rms(v, g):
    ms = jnp.mean(v * v, axis=-1, keepdims=True)
    return v * lax.rsqrt(ms + EPS) * g


def _causal_conv(x, prev8, w_ref, b_ref, col):
    rows8 = lax.broadcasted_iota(jnp.int32, prev8.shape, 0)
    acc = x * w_ref[CONV_WIDTH - 1:CONV_WIDTH, col] + b_ref[:, col]
    for k in range(1, CONV_WIDTH):
        xr = pltpu.roll(x, k, 0)
        top = jnp.where(rows8 < k, pltpu.roll(prev8, k, 0), xr[0:SUBLANES])
        shifted = jnp.concatenate([top, xr[SUBLANES:]], axis=0)
        acc = acc + shifted * w_ref[CONV_WIDTH - 1 - k:CONV_WIDTH - k, col]
    return acc


def _in_proj_kernel(x_ref, g_ref, w_ref, wdt_ref, o_ref, odt_ref, h_ref):
    @pl.when(pl.program_id(1) == 0)
    def _():
        h = _rms(x_ref[...], g_ref[...]).astype(BF16)
        h_ref[...] = h
        odt_ref[...] = jnp.dot(h, wdt_ref[...], preferred_element_type=F32)

    o_ref[...] = jnp.dot(h_ref[...], w_ref[...], preferred_element_type=F32)


def _in_proj(x2, g, w_main, w_dt):
    m = x2.shape[0]
    return pl.pallas_call(
        _in_proj_kernel,
        grid=(m // IN_TM, N_MAIN // IN_TN),
        in_specs=[
            pl.BlockSpec((IN_TM, D_MODEL), lambda i, j: (i, 0)),
            pl.BlockSpec((1, D_MODEL), lambda i, j: (0, 0)),
            pl.BlockSpec((D_MODEL, IN_TN), lambda i, j: (0, j)),
            pl.BlockSpec((D_MODEL, LANES), lambda i, j: (0, 0)),
        ],
        out_specs=[
            pl.BlockSpec((IN_TM, IN_TN), lambda i, j: (i, j)),
            pl.BlockSpec((IN_TM, LANES), lambda i, j: (i, 0)),
        ],
        out_shape=[
            jax.ShapeDtypeStruct((m, N_MAIN), F32),
            jax.ShapeDtypeStruct((m, LANES), F32),
        ],
        scratch_shapes=[pltpu.VMEM((IN_TM, D_MODEL), BF16)],
        compiler_params=pltpu.CompilerParams(
            dimension_semantics=("arbitrary", "arbitrary"), vmem_limit_bytes=VMEM_LIMIT),
        name="in_proj",
    )(x2, g, w_main, w_dt)


def _ssd_kernel(z_ref, xs_ref, bc_ref, dt_ref, cw_ref, cb_ref, dtb_ref, alog_ref, dskip_ref,
                nw_ref, y_ref, prevx_ref, prevbc_ref, state_ref):
    q = SSD_CHUNK

    @pl.when(pl.program_id(1) == 0)
    def _():
        prevx_ref[...] = jnp.zeros_like(prevx_ref)
        prevbc_ref[...] = jnp.zeros_like(prevbc_ref)
        state_ref[...] = jnp.zeros_like(state_ref)

    dt = _softplus(dt_ref[...] + dtb_ref[...])
    adt = dt * (-jnp.exp(alog_ref[...]))
    row = lax.broadcasted_iota(jnp.int32, (q, LANES), 0)
    acs = adt
    s = 1
    while s < q:
        acs = acs + jnp.where(row >= s, pltpu.roll(acs, s, 0), 0.0)
        s *= 2
    acs_t = acs.T
    dt_t = dt.T
    eacs = jnp.exp(acs)

    li = lax.broadcasted_iota(jnp.int32, (q, q), 0)
    si = lax.broadcasted_iota(jnp.int32, (q, q), 1)
    causal = li >= si
    lane = lax.broadcasted_iota(jnp.int32, (1, LANES), 1)
    first_half = lane < SSD_HEAD_DIM

    for g in range(SSD_GROUPS):
        xcol = slice(g * GROUP_WIDTH, (g + 1) * GROUP_WIDTH)
        bcol = slice(g * SSD_STATE, (g + 1) * SSD_STATE)
        ccol = slice(D_SSD // 2 + g * SSD_STATE, D_SSD // 2 + (g + 1) * SSD_STATE)
        xcol_bc = slice(D_SSD + g * SSD_STATE, D_SSD + (g + 1) * SSD_STATE)
        ccol_bc = slice(D_SSD + D_SSD // 2 + g * SSD_STATE, D_SSD + D_SSD // 2 + (g + 1) * SSD_STATE)

        xg = _causal_conv(xs_ref[:, xcol], prevx_ref[:, xcol], cw_ref, cb_ref, xcol)
        xg = xg * _sigmoid(xg)
        bm = _causal_conv(bc_ref[:, bcol], prevbc_ref[:, bcol], cw_ref, cb_ref, xcol_bc)
        bm = bm * _sigmoid(bm)
        cm = _causal_conv(bc_ref[:, ccol], prevbc_ref[:, ccol], cw_ref, cb_ref, ccol_bc)
        cm = cm * _sigmoid(cm)

        scores = lax.dot_general(cm.astype(BF16), bm.astype(BF16), (((1,), (1,)), ((), ())),
                                 preferred_element_type=F32)
        bm_t = bm.T

        y_pairs = []
        for pair in range(HEADS_PER_GROUP // 2):
            pcol = slice(pair * LANES, (pair + 1) * LANES)
            x_pair = xg[:, pcol].astype(BF16)
            s_pair = state_ref[g, :, pcol]
            s_pair_b = s_pair.astype(BF16)
            y_acc = None
            s_acc = None
            cd_rows = []
            for half in range(2):
                h = g * HEADS_PER_GROUP + pair * 2 + half
                keep = first_half if half == 0 else jnp.logical_not(first_half)
                acs_col = acs[:, h:h + 1]
                acs_row = acs_t[h:h + 1, :]
                dt_row = dt_t[h:h + 1, :]
                decay = jnp.exp(jnp.where(causal, acs_col - acs_row, -jnp.inf))
                m_h = (scores * decay * dt_row).astype(BF16)
                ce_h = (cm * eacs[:, h:h + 1]).astype(BF16)
                lhs = jnp.concatenate([m_h, ce_h], axis=1)
                zero_b = jnp.zeros((), BF16)
                rhs = jnp.concatenate([jnp.where(keep, x_pair, zero_b),
                                       jnp.where(keep, s_pair_b, zero_b)], axis=0)
                y_h = jnp.dot(lhs, rhs, preferred_element_type=F32)
                y_acc = y_h if y_acc is None else y_acc + y_h

                last = acs_t[h:h + 1, q - 1:q]
                w_row = jnp.exp(last - acs_row) * dt_row
                bw = (bm_t * w_row).astype(BF16)
                s_h = jnp.dot(bw, jnp.where(keep, x_pair, zero_b), preferred_element_type=F32)
                s_acc = s_h if s_acc is None else s_acc + s_h
                cd_rows.append(jnp.exp(last))
            chunk_decay = jnp.where(first_half, cd_rows[0], cd_rows[1])
            state_ref[g, :, pcol] = s_pair * chunk_decay + s_acc
            y_pairs.append(y_acc)

        y = jnp.concatenate(y_pairs, axis=1) + xg * dskip_ref[:, xcol]
        zg = z_ref[:, xcol]
        y = y * (zg * _sigmoid(zg))
        y_ref[:, xcol] = _rms(y, nw_ref[:, xcol]).astype(y_ref.dtype)

    prevx_ref[...] = xs_ref[q - SUBLANES:q, :]
    prevbc_ref[...] = bc_ref[q - SUBLANES:q, :]


def _ssd(proj, dt_raw, cw, cb, dtb, alog, dskip, nw, batch, seq):
    m = proj.shape[0]
    nc = seq // SSD_CHUNK
    row_map = lambda col: (lambda b, c: (b * nc + c, col))
    const = lambda b, c: (0, 0)
    act = lambda col: pl.BlockSpec((SSD_CHUNK, D_MODEL), row_map(col))
    return pl.pallas_call(
        _ssd_kernel,
        grid=(batch, nc),
        in_specs=[
            act(COL_Z), act(COL_XS), act(COL_BC),
            pl.BlockSpec((SSD_CHUNK, LANES), row_map(0)),
            pl.BlockSpec((CONV_WIDTH, 2 * D_SSD), const),
            pl.BlockSpec((1, 2 * D_SSD), const),
            pl.BlockSpec((1, LANES), const),
            pl.BlockSpec((1, LANES), const),
            pl.BlockSpec((1, D_SSD), const),
            pl.BlockSpec((1, D_SSD), const),
        ],
        out_specs=pl.BlockSpec((SSD_CHUNK, D_SSD), row_map(0)),
        out_shape=jax.ShapeDtypeStruct((m, D_SSD), BF16),
        scratch_shapes=[
            pltpu.VMEM((SUBLANES, D_SSD), F32),
            pltpu.VMEM((SUBLANES, D_SSD), F32),
            pltpu.VMEM((SSD_GROUPS, SSD_STATE, GROUP_WIDTH), F32),
        ],
        compiler_params=pltpu.CompilerParams(
            dimension_semantics=("arbitrary", "arbitrary"), vmem_limit_bytes=VMEM_LIMIT),
        name="ssd_mixer",
    )(proj, proj, proj, dt_raw, cw, cb, dtb, alog, dskip, nw)


def _lru_kernel(gate_ref, xl_ref, cw_ref, cb_ref, wax_ref, ba_ref, bx_ref, lam_ref, nw_ref,
                y_ref, prev_ref, carry_ref, hg_ref):
    t = LRU_TIME

    @pl.when(pl.program_id(1) == 0)
    def _():
        prev_ref[...] = jnp.zeros_like(prev_ref)
        carry_ref[...] = jnp.zeros_like(carry_ref)

    row8 = lax.broadcasted_iota(jnp.int32, (t, LANES), 0) & (SUBLANES - 1)
    ssq = jnp.zeros((t, LANES), F32)
    for j in range(LRU_HEADS):
        col = slice(j * LRU_BLOCK, (j + 1) * LRU_BLOCK)
        xl = _causal_conv(xl_ref[:, col], prev_ref[:, col], cw_ref, cb_ref, col)
        pre = jnp.dot(xl.astype(BF16), wax_ref[j], preferred_element_type=F32)
        r = _sigmoid(pre[:, :LRU_BLOCK] + ba_ref[:, col])
        i = _sigmoid(pre[:, LRU_BLOCK:] + bx_ref[:, col])
        log_a = (-LRU_C * r) * _softplus(-lam_ref[:, col])
        a = jnp.exp(log_a)
        u = jnp.sqrt(jnp.tanh(-log_a) * (1.0 + a * a)) * (i * xl)

        for s in (1, 2, 4):
            a_sh = pltpu.roll(a, s, 0)
            u_sh = pltpu.roll(u, s, 0)
            m = row8 >= s
            u = jnp.where(m, a * u_sh + u, u)
            a = jnp.where(m, a * a_sh, a)
        h = carry_ref[:, col]
        hs = []
        for grp in range(t // SUBLANES):
            rows = slice(grp * SUBLANES, (grp + 1) * SUBLANES)
            hgp = u[rows] + a[rows] * h
            hs.append(hgp)
            h = hgp[SUBLANES - 1:SUBLANES]
        carry_ref[:, col] = h
        hfull = jnp.concatenate(hs, axis=0)

        hg = hfull * jax.nn.gelu(gate_ref[:, col])
        hg_ref[:, col] = hg
        ssq = ssq + hg * hg

    scale = lax.rsqrt(jnp.sum(ssq, axis=-1, keepdims=True) * (1.0 / D_LRU) + EPS)
    y_ref[...] = (hg_ref[...] * scale * nw_ref[...]).astype(y_ref.dtype)
    prev_ref[...] = xl_ref[t - SUBLANES:t, :]


def _lru(proj, cw, cb, wax, ba, bx, lam, nw, batch, seq):
    m = proj.shape[0]
    nt = seq // LRU_TIME
    row_map = lambda col: (lambda b, c: (b * nt + c, col))
    const = lambda b, c: (0, 0)
    vec = pl.BlockSpec((1, D_LRU), const)
    return pl.pallas_call(
        _lru_kernel,
        grid=(batch, nt),
        in_specs=[
            pl.BlockSpec((LRU_TIME, D_MODEL), row_map(COL_GATE)),
            pl.BlockSpec((LRU_TIME, D_MODEL), row_map(COL_XL)),
            pl.BlockSpec((CONV_WIDTH, D_LRU), const),
            vec,
            pl.BlockSpec((LRU_HEADS, LRU_BLOCK, 2 * LRU_BLOCK), lambda b, c: (0, 0, 0)),
            vec, vec, vec, vec,
        ],
        out_specs=pl.BlockSpec((LRU_TIME, D_LRU), row_map(0)),
        out_shape=jax.ShapeDtypeStruct((m, D_LRU), BF16),
        scratch_shapes=[
            pltpu.VMEM((SUBLANES, D_LRU), F32),
            pltpu.VMEM((1, D_LRU), F32),
            pltpu.VMEM((LRU_TIME, D_LRU), F32),
        ],
        compiler_params=pltpu.CompilerParams(
            dimension_semantics=("arbitrary", "arbitrary"), vmem_limit_bytes=VMEM_LIMIT),
        name="lru_mixer",
    )(proj, proj, cw, cb, wax, ba, bx, lam, nw)


def _out_proj_kernel(ys_ref, yl_ref, w_ref, x_ref, g_ref, o_ref):
    mix = jnp.dot(ys_ref[...], w_ref[0:D_SSD, :], preferred_element_type=F32)
    mix = mix + jnp.dot(yl_ref[...], w_ref[D_SSD:D_SSD + D_LRU, :], preferred_element_type=F32)
    o_ref[...] = x_ref[...] + _rms(mix, g_ref[...])


def _out_proj(y_ssd, y_lru, w_out, x2, g):
    m = x2.shape[0]
    rows = lambda i: (i, 0)
    const = lambda i: (0, 0)
    return pl.pallas_call(
        _out_proj_kernel,
        grid=(m // OUT_TM,),
        in_specs=[
            pl.BlockSpec((OUT_TM, D_SSD), rows),
            pl.BlockSpec((OUT_TM, D_LRU), rows),
            pl.BlockSpec((D_SSD + D_LRU, D_MODEL), const, pipeline_mode=pl.Buffered(1)),
            pl.BlockSpec((OUT_TM, D_MODEL), rows),
            pl.BlockSpec((1, D_MODEL), const),
        ],
        out_specs=pl.BlockSpec((OUT_TM, D_MODEL), rows),
        out_shape=jax.ShapeDtypeStruct((m, D_MODEL), F32),
        compiler_params=pltpu.CompilerParams(
            dimension_semantics=("arbitrary",), vmem_limit_bytes=VMEM_LIMIT),
        name="out_proj",
    )(y_ssd, y_lru, w_out, x2, g)


def _mlp_kernel(x_ref, g1_ref, w1_ref, w2_ref, g2_ref, o_ref, xn_ref, acc_ref):
    f = pl.program_id(1)

    @pl.when(f == 0)
    def _():
        xn_ref[...] = _rms(x_ref[...], g1_ref[...]).astype(BF16)

    hm = jnp.dot(xn_ref[...], w1_ref[...], preferred_element_type=F32)
    hm = jnp.square(jnp.maximum(hm, 0.0)).astype(BF16)
    part = jnp.dot(hm, w2_ref[...], preferred_element_type=F32)

    @pl.when(f == 0)
    def _():
        acc_ref[...] = part

    @pl.when(f > 0)
    def _():
        acc_ref[...] += part

    @pl.when(f == pl.num_programs(1) - 1)
    def _():
        o_ref[...] = x_ref[...] + _rms(acc_ref[...], g2_ref[...])


def _mlp(x1, g1, w1, w2, g2):
    m = x1.shape[0]
    return pl.pallas_call(
        _mlp_kernel,
        grid=(m // MLP_TM, D_FF // MLP_TF),
        in_specs=[
            pl.BlockSpec((MLP_TM, D_MODEL), lambda i, f: (i, 0)),
            pl.BlockSpec((1, D_MODEL), lambda i, f: (0, 0)),
            pl.BlockSpec((D_MODEL, MLP_TF), lambda i, f: (0, f)),
            pl.BlockSpec((MLP_TF, D_MODEL), lambda i, f: (f, 0)),
            pl.BlockSpec((1, D_MODEL), lambda i, f: (0, 0)),
        ],
        out_specs=pl.BlockSpec((MLP_TM, D_MODEL), lambda i, f: (i, 0)),
        out_shape=jax.ShapeDtypeStruct((m, D_MODEL), F32),
        scratch_shapes=[
            pltpu.VMEM((MLP_TM, D_MODEL), BF16),
            pltpu.VMEM((MLP_TM, D_MODEL), F32),
        ],
        compiler_params=pltpu.CompilerParams(
            dimension_semantics=("arbitrary", "arbitrary"), vmem_limit_bytes=VMEM_LIMIT),
        name="mlp",
    )(x1, g1, w1, w2, g2)


def _layer(x2, batch, seq, pre_mix_norm, w_in, ssd_conv_w, ssd_conv_b, ssd_dt_bias, ssd_a_log, ssd_d,
           ssd_norm, lru_conv_w, lru_conv_b, lru_w_a, lru_b_a, lru_w_x, lru_b_x, lru_lambda,
           lru_norm, w_out, post_mix_norm, pre_mlp_norm, w_mlp_in, w_mlp_out, post_mlp_norm):
    d_xbc = D_SSD + 2 * SSD_GROUPS * SSD_STATE
    o_xbc = D_SSD
    o_dt = o_xbc + d_xbc
    o_gate = o_dt + SSD_HEADS
    o_xl = o_gate + D_LRU
    w_main = jnp.concatenate(
        [w_in[:, :o_dt], w_in[:, o_gate:o_xl], w_in[:, o_xl:o_xl + D_LRU]], axis=1).astype(BF16)
    w_dt = jnp.pad(w_in[:, o_dt:o_gate], ((0, 0), (0, LANES - SSD_HEADS))).astype(BF16)
    row = lambda v: v.reshape(1, -1).astype(F32)
    pad_heads = lambda v: jnp.pad(v.astype(F32), (0, LANES - SSD_HEADS)).reshape(1, LANES)

    proj, dt_raw = _in_proj(x2, row(pre_mix_norm), w_main, w_dt)

    y_ssd = _ssd(proj, dt_raw, ssd_conv_w.astype(F32), row(ssd_conv_b), pad_heads(ssd_dt_bias),
                 pad_heads(ssd_a_log), row(jnp.repeat(ssd_d, SSD_HEAD_DIM)), row(ssd_norm), batch, seq)

    wax = jnp.concatenate([lru_w_a, lru_w_x], axis=-1).astype(BF16)
    y_lru = _lru(proj, lru_conv_w.astype(F32), row(lru_conv_b), wax, row(lru_b_a), row(lru_b_x),
                 row(lru_lambda), row(lru_norm), batch, seq)

    x1 = _out_proj(y_ssd, y_lru, w_out.astype(BF16), x2, row(post_mix_norm))
    return _mlp(x1, row(pre_mlp_norm), w_mlp_in.astype(BF16), w_mlp_out.astype(BF16), row(post_mlp_norm))


def kernel(x, pre_mix_norm, w_in, ssd_conv_w, ssd_conv_b, ssd_dt_bias, ssd_a_log, ssd_d, ssd_norm,
           lru_conv_w, lru_conv_b, lru_w_a, lru_b_a, lru_w_x, lru_b_x, lru_lambda, lru_norm, w_out,
           post_mix_norm, pre_mlp_norm, w_mlp_in, w_mlp_out, post_mlp_norm):
    batch, seq, d = x.shape
    x2 = x.reshape(batch * seq, d)
    params = (pre_mix_norm, w_in, ssd_conv_w, ssd_conv_b, ssd_dt_bias, ssd_a_log, ssd_d, ssd_norm,
              lru_conv_w, lru_conv_b, lru_w_a, lru_b_a, lru_w_x, lru_b_x, lru_lambda, lru_norm, w_out,
              post_mix_norm, pre_mlp_norm, w_mlp_in, w_mlp_out, post_mlp_norm)
    for li in range(pre_mix_norm.shape[0]):
        x2 = _layer(x2, batch, seq, *(p[li] for p in params))
    return x2.reshape(batch, seq, d)
```
